```python
import jax
import jax.numpy as jnp
from jax import lax
import numpy as np

D_MODEL = 2048
BATCH = 1
SEQ = 8192
DEPTH = 4

N_EVEN = (DEPTH + 1) // 2
N_ODD = DEPTH // 2
MIX_WIDTH = D_MODEL

POOL_WIDTH = D_MODEL // 2
POOL_WINDOWS = (2, 4, 8, 16)
POOL_GROUPS = len(POOL_WINDOWS)
POOL_GROUP_DIM = POOL_WIDTH // POOL_GROUPS
HGRN_WIDTH = D_MODEL // 2
HGRN_HEAD_DIM = 128
HGRN_HEADS = HGRN_WIDTH // HGRN_HEAD_DIM
GLA_HEADS = 4
GLA_VALUE_WIDTH = D_MODEL // 2
GLA_KEY_WIDTH = GLA_VALUE_WIDTH // 2
GLA_VALUE_DIM = GLA_VALUE_WIDTH // GLA_HEADS
GLA_KEY_DIM = GLA_KEY_WIDTH // GLA_HEADS
GLA_GATE_RANK = 16
GLA_GATE_TEMP = 16.0
CHUNK = 64
SB_WIDTH = D_MODEL // 2
SB_HEAD_DIM = 128
SB_HEADS = SB_WIDTH // SB_HEAD_DIM
SB_BLOCK = 128
N_GROUPS = 4
EXPERTS_PER_GROUP = 8
N_EXPERTS = N_GROUPS * EXPERTS_PER_GROUP
TOP_K_IN_GROUP = 2
EXPERT_HIDDEN = D_MODEL // 4
MOE_ROW_BLOCK = 128

AB_SPLITS = (POOL_WIDTH, HGRN_WIDTH, HGRN_WIDTH, HGRN_WIDTH, HGRN_WIDTH)
CD_SPLITS = (GLA_KEY_WIDTH, GLA_KEY_WIDTH, GLA_VALUE_WIDTH, GLA_VALUE_WIDTH, GLA_GATE_RANK,
             SB_WIDTH, SB_WIDTH, SB_WIDTH)
AB_IN_WIDTH = sum(AB_SPLITS)
CD_IN_WIDTH = sum(CD_SPLITS)

DEEPNORM_ALPHA = (2 * DEPTH) ** 0.25
DEEPNORM_BETA = (8 * DEPTH) ** -0.25
NORM_EPS = 1e-5

kernel_name = 'hybrid_pool_hgrn2_gla_stickbreak_hmoe'


def split_last(a, sizes):
    parts, start = [], 0
    for s in sizes:
        parts.append(a[..., start:start + s])
        start += s
    return parts


def layer_norm(x, g, b):
    xf = x.astype(jnp.float32)
    mu = jnp.mean(xf, -1, keepdims=True)
    var = jnp.mean(jnp.square(xf - mu), -1, keepdims=True)
    return ((xf - mu) * lax.rsqrt(var + NORM_EPS) * g + b).astype(x.dtype)


def rms_norm(x, g):
    xf = x.astype(jnp.float32)
    return xf * lax.rsqrt(jnp.mean(jnp.square(xf), -1, keepdims=True) + NORM_EPS) * g


def multiscale_causal_pool(a):
    T = a.shape[1]
    af = a.astype(jnp.float32)
    cs = jnp.cumsum(af, axis=1)
    pos = jnp.arange(1, T + 1, dtype=jnp.float32)
    outs = []
    for g, w in enumerate(POOL_WINDOWS):
        csg = cs[:, :, g]
        prev = jnp.pad(csg, ((0, 0), (w, 0), (0, 0)))[:, :T]
        mean = (csg - prev) / jnp.minimum(pos, float(w))[None, :, None]
        outs.append(mean - af[:, :, g])
    return jnp.stack(outs, axis=2)


def chunk_gated_linear_attention(q, k, v, log_f):
    B, T, H, Dk = q.shape
    Dv = v.shape[-1]
    n = T // CHUNK

    def to_chunks(a):
        return a.astype(jnp.float32).reshape(B, n, CHUNK, H, a.shape[-1]).transpose(1, 0, 3, 2, 4)

    qc, kc, vc, gc = to_chunks(q), to_chunks(k), to_chunks(v), to_chunks(log_f)
    causal = jnp.tril(jnp.ones((CHUNK, CHUNK), dtype=bool))

    def step(S, inp):
        qi, ki, vi, gi = inp
        b = jnp.cumsum(gi, axis=-2)
        b_last = b[..., -1:, :]
        inter = jnp.einsum('bhcd,bhde->bhce', qi * jnp.exp(b), S)
        rel = jnp.where(causal[:, :, None], b[..., :, None, :] - b[..., None, :, :], -jnp.inf)
        scores = jnp.einsum('bhid,bhjd,bhijd->bhij', qi, ki, jnp.exp(rel))
        intra = jnp.einsum('bhij,bhje->bhie', scores, vi)
        S_new = jnp.exp(b_last[..., 0, :])[..., None] * S + jnp.einsum(
            'bhcd,bhce->bhde', ki * jnp.exp(b_last - b), vi)
        return S_new, inter + intra

    S0 = jnp.zeros((B, H, Dk, Dv), jnp.float32)
    _, out = lax.scan(step, S0, (qc, kc, vc, gc))
    return out.transpose(1, 0, 3, 2, 4).reshape(B, T, H, Dv)


def stick_breaking_attention(q, k, v):
    B, T, H, dh = q.shape
    qh = jnp.swapaxes(q.astype(jnp.float32), 1, 2) * (dh ** -0.5)
    kh = jnp.swapaxes(k.astype(jnp.float32), 1, 2)
    vh = jnp.swapaxes(v.astype(jnp.float32), 1, 2)
    outs = []
    for blk in range(T // SB_BLOCK):
        t0 = blk * SB_BLOCK
        t1 = t0 + SB_BLOCK
        z = jnp.einsum('bhqd,bhkd->bhqk', qh[:, :, t0:t1], kh[:, :, :t1])
        mask = jnp.arange(t1)[None, :] < (t0 + jnp.arange(SB_BLOCK))[:, None]
        log_keep = jnp.where(mask, jax.nn.log_sigmoid(-z), 0.0)
        log_after = lax.cumsum(log_keep, axis=3, reverse=True) - log_keep
        w = jnp.where(mask, jnp.exp(jax.nn.log_sigmoid(z) + log_after), 0.0)
        outs.append(jnp.einsum('bhqk,bhkd->bhqd', w, vh[:, :, :t1]))
    return jnp.swapaxes(jnp.concatenate(outs, axis=2), 1, 2)


def hgrn_lower_bounds(logits):
    cum = jnp.cumsum(jax.nn.softmax(logits.astype(jnp.float32), axis=0), axis=0)
    return cum - cum[0:1]


def pool_hgrn_mixer(u, w_in, pool_w, pool_scale, lower_bound, norm_g, w_out):
    B, T, _ = u.shape
    a_in, q, f_pre, i_in, g_out = split_last(u @ w_in, AB_SPLITS)
    a = a_in.reshape(B, T, POOL_GROUPS, POOL_GROUP_DIM)
    pooled = multiscale_causal_pool(a).astype(u.dtype)
    a_out = jnp.einsum('btgc,gcd->btgd', pooled, pool_w).reshape(B, T, POOL_WIDTH) * pool_scale
    heads = (B, T, HGRN_HEADS, HGRN_HEAD_DIM)
    lb = lower_bound.astype(jnp.float32).reshape(HGRN_HEADS, HGRN_HEAD_DIM)
    z = f_pre.astype(jnp.float32).reshape(heads)
    log_f = jnp.logaddexp(jnp.log(lb), jnp.log1p(-lb) + jax.nn.log_sigmoid(z))
    k = (1.0 - lb) * jax.nn.sigmoid(-z)
    o = chunk_gated_linear_attention(jax.nn.silu(q.reshape(heads)), k, i_in.reshape(heads), log_f)
    o = rms_norm(o, norm_g.reshape(HGRN_HEADS, HGRN_HEAD_DIM)) * jax.nn.silu(
        g_out.astype(jnp.float32).reshape(heads))
    mix = jnp.concatenate([a_out, o.reshape(B, T, HGRN_WIDTH).astype(u.dtype)], axis=-1)
    return mix @ w_out


def gla_stickbreak_mixer(u, w_in, gate_w, gate_b, norm_g, w_out):
    B, T, _ = u.shape
    cq, ck, cv, cg, c_lr, dq, dk, dv = split_last(u @ w_in, CD_SPLITS)
    kshape = (B, T, GLA_HEADS, GLA_KEY_DIM)
    vshape = (B, T, GLA_HEADS, GLA_VALUE_DIM)
    log_a = jax.nn.log_sigmoid((c_lr @ gate_w).astype(jnp.float32) + gate_b) / GLA_GATE_TEMP
    o_c = chunk_gated_linear_attention(cq.reshape(kshape) * (GLA_KEY_DIM ** -0.5), ck.reshape(kshape),
                                       cv.reshape(vshape), log_a.reshape(kshape))
    o_c = rms_norm(o_c, norm_g.reshape(GLA_HEADS, GLA_VALUE_DIM)) * jax.nn.silu(
        cg.astype(jnp.float32).reshape(vshape))
    sshape = (B, T, SB_HEADS, SB_HEAD_DIM)
    o_d = stick_breaking_attention(dq.reshape(sshape), dk.reshape(sshape), dv.reshape(sshape))
    mix = jnp.concatenate([o_c.reshape(B, T, GLA_VALUE_WIDTH), o_d.reshape(B, T, SB_WIDTH)], axis=-1)
    return mix.astype(u.dtype) @ w_out


def hierarchical_moe(u, w_rg, b_rg, w_re, b_re, w_gate, w_up, w_down):
    B, T, D = u.shape
    N = B * T
    K = TOP_K_IN_GROUP
    h = u.reshape(N, D)
    g_prob = jax.nn.softmax((h @ w_rg).astype(jnp.float32) + b_rg, axis=-1)
    g_p, g_idx = lax.top_k(g_prob, 1)
    e_all = ((h @ w_re).astype(jnp.float32) + b_re).reshape(N, N_GROUPS, EXPERTS_PER_GROUP)
    sel = jnp.broadcast_to(g_idx[:, :, None], (N, 1, EXPERTS_PER_GROUP))
    e_prob = jax.nn.softmax(jnp.take_along_axis(e_all, sel, axis=1)[:, 0], axis=-1)
    e_p, e_idx = lax.top_k(e_prob, K)
    weights = g_p * e_p / jnp.sum(e_p, axis=-1, keepdims=True)
    expert_id = g_idx * EXPERTS_PER_GROUP + e_idx

    S = N * K
    flat_e = expert_id.reshape(S)
    flat_tok = jnp.arange(S, dtype=jnp.int32) // K
    order = jnp.argsort(flat_e)
    sorted_e = flat_e[order]
    counts = jnp.bincount(flat_e, length=N_EXPERTS)
    padded = (counts + MOE_ROW_BLOCK - 1) // MOE_ROW_BLOCK * MOE_ROW_BLOCK
    pad_end = jnp.cumsum(padded)
    pad_start = pad_end - padded
    start = jnp.cumsum(counts) - counts
    dest = pad_start[sorted_e] + jnp.arange(S, dtype=jnp.int32) - start[sorted_e]
    n_blocks = (S + N_EXPERTS * (MOE_ROW_BLOCK - 1) + MOE_ROW_BLOCK - 1) // MOE_ROW_BLOCK
    rows_tok = jnp.full((n_blocks * MOE_ROW_BLOCK,), N, dtype=jnp.int32).at[dest].set(flat_tok[order])
    block_e = jnp.minimum(jnp.searchsorted(pad_end, jnp.arange(n_blocks) * MOE_ROW_BLOCK, side='right'),
                          N_EXPERTS - 1)
    h_pad = jnp.concatenate([h, jnp.zeros((1, D), h.dtype)], axis=0)
    xb = h_pad[rows_tok].reshape(n_blocks, MOE_ROW_BLOCK, D)

    def expert_block(args):
        xi, e = args
        return (jax.nn.silu(xi @ w_gate[e]) * (xi @ w_up[e])) @ w_down[e]

    yb = lax.map(expert_block, (xb, block_e)).reshape(n_blocks * MOE_ROW_BLOCK, D)
    slot_row = jnp.zeros((S,), jnp.int32).at[order].set(dest)
    y = jnp.sum(yb[slot_row].reshape(N, K, D) * weights[..., None].astype(yb.dtype), axis=1)
    return y.reshape(B, T, D)


def _normal(k, shape, scale):
    return jax.random.normal(k, shape, jnp.float32) * scale


def setup_inputs(seed: int = 0) -> dict:
    key = jax.random.key(seed)
    ks = jax.random.split(key, 24)
    D = D_MODEL
    return {
        'x': _normal(ks[0], (BATCH, SEQ, D), 1.0),
        'c': _normal(ks[1], (BATCH, D), 1.0),
        'ada_w': _normal(ks[2], (DEPTH, D, 6 * D), 0.3 * D ** -0.5),
        'ada_b': _normal(ks[3], (DEPTH, 6 * D), 0.02),
        'ln_g': 1.0 + _normal(ks[4], (DEPTH, 2, D), 0.02),
        'ln_b': _normal(ks[5], (DEPTH, 2, D), 0.02),
        'w_in_ab': _normal(ks[6], (N_EVEN, D, AB_IN_WIDTH), D ** -0.5),
        'pool_w': _normal(ks[7], (N_EVEN, POOL_GROUPS, POOL_GROUP_DIM, POOL_GROUP_DIM), POOL_GROUP_DIM ** -0.5),
        'pool_scale': 1.0 + _normal(ks[8], (N_EVEN, POOL_WIDTH), 0.1),
        'hgrn_lb_logits': _normal(ks[9], (N_EVEN, HGRN_WIDTH), 0.5),
        'hgrn_norm_g': 1.0 + _normal(ks[10], (N_EVEN, HGRN_WIDTH), 0.02),
        'w_out_ab': _normal(ks[11], (N_EVEN, MIX_WIDTH, D), DEEPNORM_BETA * MIX_WIDTH ** -0.5),
        'w_in_cd': _normal(ks[12], (N_ODD, D, CD_IN_WIDTH), D ** -0.5),
        'gla_gate_w': _normal(ks[13], (N_ODD, GLA_GATE_RANK, GLA_KEY_WIDTH), GLA_GATE_RANK ** -0.5),
        'gla_gate_b': _normal(ks[14], (N_ODD, GLA_KEY_WIDTH), 0.1),
        'gla_norm_g': 1.0 + _normal(ks[15], (N_ODD, GLA_VALUE_WIDTH), 0.02),
        'w_out_cd': _normal(ks[16], (N_ODD, MIX_WIDTH, D), DEEPNORM_BETA * MIX_WIDTH ** -0.5),
        'router_group_w': _normal(ks[17], (DEPTH, D, N_GROUPS), D ** -0.5),
        'router_group_b': _normal(ks[18], (DEPTH, N_GROUPS), 0.01),
        'router_expert_w': _normal(ks[19], (DEPTH, D, N_EXPERTS), D ** -0.5),
        'router_expert_b': _normal(ks[20], (DEPTH, N_EXPERTS), 0.01),
        'expert_w_gate': _normal(ks[21], (DEPTH, N_EXPERTS, D, EXPERT_HIDDEN), D ** -0.5),
        'expert_w_up': _normal(ks[22], (DEPTH, N_EXPERTS, D, EXPERT_HIDDEN), D ** -0.5),
        'expert_w_down': _normal(ks[23], (DEPTH, N_EXPERTS, EXPERT_HIDDEN, D), DEEPNORM_BETA * EXPERT_HIDDEN ** -0.5),
    }


def reference(x, c, ada_w, ada_b, ln_g, ln_b, w_in_ab, pool_w, pool_scale, hgrn_lb_logits, hgrn_norm_g,
              w_out_ab, w_in_cd, gla_gate_w, gla_gate_b, gla_norm_g, w_out_cd, router_group_w,
              router_group_b, router_expert_w, router_expert_b, expert_w_gate, expert_w_up, expert_w_down):
    lower_bounds = hgrn_lower_bounds(hgrn_lb_logits)
    cond = jax.nn.silu(c)
    for layer in range(DEPTH):
        mod = cond @ ada_w[layer] + ada_b[layer]
        shift_m, scale_m, gate_m, shift_f, scale_f, gate_f = jnp.split(mod[:, None, :], 6, axis=-1)
        i = layer // 2
        u = x * (1.0 + scale_m) + shift_m
        if layer % 2 == 0:
            y = pool_hgrn_mixer(u, w_in_ab[i], pool_w[i], pool_scale[i], lower_bounds[i], hgrn_norm_g[i],
                                w_out_ab[i])
        else:
            y = gla_stickbreak_mixer(u, w_in_cd[i], gla_gate_w[i], gla_gate_b[i], gla_norm_g[i], w_out_cd[i])
        x = layer_norm(DEEPNORM_ALPHA * x + (1.0 + gate_m) * y, ln_g[layer, 0], ln_b[layer, 0])
        u = x * (1.0 + scale_f) + shift_f
        y = hierarchical_moe(u, router_group_w[layer], router_group_b[layer], router_expert_w[layer],
                             router_expert_b[layer], expert_w_gate[layer], expert_w_up[layer],
                             expert_w_down[layer])
        x = layer_norm(DEEPNORM_ALPHA * x + (1.0 + gate_f) * y, ln_g[layer, 1], ln_b[layer, 1])
    return x
```

```python
import functools
import math

import numpy as np
import jax
import jax.numpy as jnp
from jax import lax
from jax.experimental import pallas as pl
from jax.experimental.pallas import tpu as pltpu

F32 = jnp.float32
BF16 = jnp.bfloat16

LANES = 128
SUBLANES = 8
MIB = 1024 * 1024

POOL_WINDOWS = (2, 4, 8, 16)
POOL_HALO = 16
HGRN_HEAD_DIM = 128
GLA_HEADS = 4
GLA_GATE_TEMP = 16.0
SB_HEAD_DIM = 128
N_GROUPS = 4
EXPERTS_PER_GROUP = 8
N_EXPERTS = N_GROUPS * EXPERTS_PER_GROUP
MOE_ROW_BLOCK = 128
NORM_EPS = 1e-5

GLA_CHUNK = 64
SB_Q_BLOCK = 128
SB_K_BLOCK = 256
SB_SKIP_LOG = -110.0
ADA_ROWS = 64


def _cparams(semantics, vmem_mib):
    return pltpu.CompilerParams(dimension_semantics=semantics, vmem_limit_bytes=vmem_mib * MIB)


def _dot(a, b):
    return jnp.dot(a, b, preferred_element_type=F32)


def _dot_nt(a, b):
    return lax.dot_general(a, b, (((1,), (1,)), ((), ())), preferred_element_type=F32)


def _dot_tn(a, b):
    return lax.dot_general(a, b, (((0,), (0,)), ((), ())), preferred_element_type=F32)


def _split_bf16(a):
    hi = a.astype(BF16)
    lo = (a - hi.astype(F32)).astype(BF16)
    return hi, lo


def _silu(a):
    return a * jax.nn.sigmoid(a)


def _layer_norm(r, g, b):
    mu = jnp.mean(r, axis=-1, keepdims=True)
    xc = r - mu
    var = jnp.mean(xc * xc, axis=-1, keepdims=True)
    return xc * lax.rsqrt(var + NORM_EPS) * g + b


def _ada_mod_kernel(c_ref, w_ref, b_ref, o_ref):
    d = c_ref.shape[0]
    bn = o_ref.shape[-1]

    def body(r, acc):
        rows = pl.ds(pl.multiple_of(r * ADA_ROWS, ADA_ROWS), ADA_ROWS)
        cond = _silu(c_ref[rows, :])
        p = cond * w_ref[0, rows, :]
        return acc + jnp.sum(p.reshape(ADA_ROWS // SUBLANES, SUBLANES, bn), axis=0)

    acc = lax.fori_loop(0, d // ADA_ROWS, body, jnp.zeros((SUBLANES, bn), F32))
    o_ref[0] = jnp.sum(acc, axis=0, keepdims=True) + b_ref[0]


def _ada_mod(c, ada_w, ada_b):
    depth, d, n6 = ada_w.shape
    bn = 1024
    return pl.pallas_call(
        _ada_mod_kernel,
        grid=(depth, n6 // bn),
        in_specs=[pl.BlockSpec((d, 1), lambda l, j: (0, 0)),
                  pl.BlockSpec((1, d, bn), lambda l, j: (l, 0, j)),
                  pl.BlockSpec((1, 1, bn), lambda l, j: (l, 0, j))],
        out_specs=pl.BlockSpec((1, 1, bn), lambda l, j: (l, 0, j)),
        out_shape=jax.ShapeDtypeStruct((depth, 1, n6), F32),
        compiler_params=_cparams(("arbitrary", "arbitrary"), 32),
    )(c.reshape(d, 1), ada_w, ada_b.reshape(depth, 1, n6))


def _mod_matmul_kernel(x_ref, sc_ref, sh_ref, w_ref, o_ref, u_ref):
    @pl.when(pl.program_id(1) == 0)
    def _():
        u_ref[...] = (x_ref[...] * (1.0 + sc_ref[...]) + sh_ref[...]).astype(BF16)

    o_ref[...] = _dot(u_ref[...], w_ref[...])


def _mod_matmul2_kernel(x_ref, sc_ref, sh_ref, w_ref, w2_ref, o_ref, o2_ref, u_ref):
    @pl.when(pl.program_id(1) == 0)
    def _():
        u = (x_ref[...] * (1.0 + sc_ref[...]) + sh_ref[...]).astype(BF16)
        u_ref[...] = u
        o2_ref[...] = _dot(u, w2_ref[...])

    o_ref[...] = _dot(u_ref[...], w_ref[...])


def _mod_matmul(x, scale, shift, w, w2=None):
    m, d = x.shape
    n = w.shape[1]
    tm = min(1024, m)
    tn = 512
    row = pl.BlockSpec((1, d), lambda i, j: (0, 0))
    in_specs = [pl.BlockSpec((tm, d), lambda i, j: (i, 0)), row, row,
                pl.BlockSpec((d, tn), lambda i, j: (0, j))]
    out_specs = pl.BlockSpec((tm, tn), lambda i, j: (i, j))
    out_shape = jax.ShapeDtypeStruct((m, n), F32)
    args = [x, scale, shift, w]
    kern = _mod_matmul_kernel
    if w2 is not None:
        n2 = w2.shape[1]
        in_specs.append(pl.BlockSpec((d, n2), lambda i, j: (0, 0)))
        out_specs = [out_specs, pl.BlockSpec((tm, n2), lambda i, j: (i, 0))]
        out_shape = [out_shape, jax.ShapeDtypeStruct((m, n2), F32)]
        args.append(w2)
        kern = _mod_matmul2_kernel
    return pl.pallas_call(
        kern,
        grid=(m // tm, n // tn),
        in_specs=in_specs,
        out_specs=out_specs,
        out_shape=out_shape,
        scratch_shapes=[pltpu.VMEM((tm, d), BF16)],
        compiler_params=_cparams(("arbitrary", "arbitrary"), 48),
    )(*args)


def _pool_kernel(a_ref, halo_ref, pw_ref, ps_ref, o_ref, buf_ref, *, tm, gd, sub):
    i = pl.program_id(0)
    buf_ref[0:POOL_HALO, :] = jnp.where(i > 0, halo_ref[...], 0.0)
    buf_ref[POOL_HALO:POOL_HALO + tm, :] = a_ref[...]
    for r0 in range(0, tm, sub):
        pos = (i * tm + r0 + 1 + lax.broadcasted_iota(jnp.int32, (sub, 1), 0)).astype(F32)
        for g, w in enumerate(POOL_WINDOWS):
            cols = slice(g * gd, (g + 1) * gd)
            base = POOL_HALO + r0
            a = buf_ref[base:base + sub, cols]
            s = a
            for k in range(1, w):
                s = s + buf_ref[base - k:base - k + sub, cols]
            pooled = s / jnp.minimum(pos, float(w)) - a
            y = _dot(pooled.astype(BF16), pw_ref[g].astype(BF16))
            o_ref[r0:r0 + sub, cols] = (y * ps_ref[:, cols]).astype(o_ref.dtype)


def _pool_mixer(p, pool_w, pool_scale):
    t = p.shape[0]
    groups, gd, _ = pool_w.shape
    width = groups * gd
    tm = min(512, t)
    sub = 128
    kern = functools.partial(_pool_kernel, tm=tm, gd=gd, sub=sub)
    return pl.pallas_call(
        kern,
        grid=(t // tm,),
        in_specs=[pl.BlockSpec((tm, width), lambda i: (i, 0)),
                  pl.BlockSpec((POOL_HALO, width), lambda i: (jnp.maximum(i * (tm // POOL_HALO) - 1, 0), 0)),
                  pl.BlockSpec((groups, gd, gd), lambda i: (0, 0, 0)),
                  pl.BlockSpec((1, width), lambda i: (0, 0))],
        out_specs=pl.BlockSpec((tm, width), lambda i: (i, 0)),
        out_shape=jax.ShapeDtypeStruct((t, width), BF16),
        scratch_shapes=[pltpu.VMEM((POOL_HALO + tm, width), F32)],
        compiler_params=_cparams(("arbitrary",), 32),
    )(p, p, pool_w, pool_scale.reshape(1, width))


@functools.lru_cache(maxsize=None)
def _gla_constants(c):
    levels = int(math.log2(c))
    idx = np.arange(c)
    r = idx[None, :]
    i = idx[:, None]
    mats = [r <= i, r > i]
    roles, masks = [], []
    for l in range(levels):
        s = 1 << l
        mid = ((idx // (2 * s)) * (2 * s) + s)[:, None]
        upper = (idx % (2 * s)) >= s
        mats.append(np.where(upper[:, None], (r >= mid) & (r <= i), (r > i) & (r < mid)))
        roles.append(np.broadcast_to(upper[:, None], (c, LANES)))
        same = (idx[:, None] // (2 * s)) == (idx[None, :] // (2 * s))
        masks.append(same & upper[:, None] & ~upper[None, :])
    masks.append(np.eye(c, dtype=bool))
    wexp = np.concatenate(mats, axis=0).astype(np.float32)
    return (jnp.asarray(wexp, BF16), jnp.asarray(np.stack(roles), F32), jnp.asarray(np.stack(masks), F32))


def _chunk_gla(q, k, v, g, st_ref, h, wexp, roles, masks, c, levels):
    g_hi, g_lo = _split_bf16(g)
    ex = jnp.exp(_dot(wexp, g_hi) + _dot(wexp, g_lo))
    eb = ex[0:c]
    er = ex[c:2 * c]
    a = jnp.where(masks[levels], _dot_nt(q.astype(BF16), k.astype(BF16)), 0.0)
    for l in range(levels):
        xl = (jnp.where(roles[l], q, k) * ex[(2 + l) * c:(3 + l) * c]).astype(BF16)
        a = a + jnp.where(masks[l], _dot_nt(xl, xl), 0.0)
    st = st_ref[h]
    o = _dot_nt((q * eb).astype(BF16), st.astype(BF16)) + _dot(a.astype(BF16), v.astype(BF16))
    st_ref[h] = st * eb[c - 1:c, :] + _dot_tn(v.astype(BF16), (k * er).astype(BF16))
    return o


def _gated_rms(o, norm_g, gate):
    return o * lax.rsqrt(jnp.mean(o * o, axis=-1, keepdims=True) + NORM_EPS) * norm_g * _silu(gate)


def _hgrn_kernel(q_ref, f_ref, i_ref, go_ref, lbl_ref, ng_ref, wexp_ref, roles_ref, masks_ref, o_ref, st_ref,
                 *, layer_i, c, levels, heads, dh):
    @pl.when(pl.program_id(0) == 0)
    def _():
        st_ref[...] = jnp.zeros_like(st_ref)

    rows = [lbl_ref[r:r + 1, :] for r in range(lbl_ref.shape[0])]
    mx = functools.reduce(jnp.maximum, rows)
    es = [jnp.exp(r - mx) for r in rows]
    tot = functools.reduce(jnp.add, es)
    sm = [e / tot for e in es]
    cum = [functools.reduce(jnp.add, sm[:r + 1]) for r in range(len(sm))]
    lb = cum[layer_i] - cum[0]
    log_lb = jnp.log(lb)
    log_1m = jnp.log1p(-lb)
    one_m = 1.0 - lb

    wexp = wexp_ref[...]
    roles = [roles_ref[l] > 0.5 for l in range(levels)]
    masks = [masks_ref[l] > 0.5 for l in range(levels + 1)]
    for h in range(heads):
        hs = slice(h * dh, (h + 1) * dh)
        z = f_ref[:, hs]
        g = jnp.logaddexp(log_lb[:, hs], log_1m[:, hs] + jax.nn.log_sigmoid(z))
        k = one_m[:, hs] * jax.nn.sigmoid(-z)
        q = _silu(q_ref[:, hs])
        o = _chunk_gla(q, k, i_ref[:, hs], g, st_ref, h, wexp, roles, masks, c, levels)
        o_ref[:, hs] = _gated_rms(o, ng_ref[:, hs], go_ref[:, hs]).astype(o_ref.dtype)


def _hgrn_mixer(p, lb_logits, norm_g, layer_i):
    t = p.shape[0]
    width = lb_logits.shape[1]
    heads = width // HGRN_HEAD_DIM
    c = GLA_CHUNK
    levels = int(math.log2(c))
    wexp, roles, masks = _gla_constants(c)
    col = lambda k: pl.BlockSpec((c, width), lambda i: (i, k))
    const2 = lambda a: pl.BlockSpec(a.shape, lambda i: (0,) * a.ndim)
    kern = functools.partial(_hgrn_kernel, layer_i=layer_i, c=c, levels=levels, heads=heads, dh=HGRN_HEAD_DIM)
    return pl.pallas_call(
        kern,
        grid=(t // c,),
        in_specs=[col(1), col(2), col(3), col(4), const2(lb_logits), pl.BlockSpec((1, width), lambda i: (0, 0)),
                  const2(wexp), const2(roles), const2(masks)],
        out_specs=pl.BlockSpec((c, width), lambda i: (i, 0)),
        out_shape=jax.ShapeDtypeStruct((t, width), BF16),
        scratch_shapes=[pltpu.VMEM((heads, HGRN_HEAD_DIM, HGRN_HEAD_DIM), F32)],
        compiler_params=_cparams(("arbitrary",), 32),
    )(p, p, p, p, lb_logits, norm_g.reshape(1, width), wexp, roles, masks)


def _gla_kernel(qk_ref, v_ref, go_ref, lr_ref, gw_ref, gb_ref, ng_ref, wexp_ref, roles_ref, masks_ref, o_ref, st_ref,
                *, c, levels, heads, dk, dv):
    @pl.when(pl.program_id(0) == 0)
    def _():
        st_ref[...] = jnp.zeros_like(st_ref)

    kw = heads * dk
    pre = _dot(lr_ref[...].astype(BF16), gw_ref[...].astype(BF16)) + gb_ref[...]
    log_a = jax.nn.log_sigmoid(pre) / GLA_GATE_TEMP
    wexp = wexp_ref[...]
    roles = [roles_ref[l] > 0.5 for l in range(levels)]
    masks = [masks_ref[l] > 0.5 for l in range(levels + 1)]
    for h in range(heads):
        ks = slice(h * dk, (h + 1) * dk)
        vs = slice(h * dv, (h + 1) * dv)
        q = qk_ref[:, ks] * (dk ** -0.5)
        k = qk_ref[:, kw + h * dk:kw + (h + 1) * dk]
        o = _chunk_gla(q, k, v_ref[:, vs], log_a[:, ks], st_ref, h, wexp, roles, masks, c, levels)
        o_ref[:, vs] = _gated_rms(o, ng_ref[:, vs], go_ref[:, vs]).astype(o_ref.dtype)


def _gla_mixer(p, lr, gate_w, gate_b, norm_g):
    t = p.shape[0]
    vw = norm_g.shape[0]
    kw = gate_w.shape[1]
    heads = GLA_HEADS
    dk, dv = kw // heads, vw // heads
    c = GLA_CHUNK
    levels = int(math.log2(c))
    wexp, roles, masks = _gla_constants(c)
    gw = jnp.zeros((lr.shape[1], kw), F32).at[:gate_w.shape[0]].set(gate_w)
    col = lambda k: pl.BlockSpec((c, vw), lambda i: (i, k))
    const2 = lambda a: pl.BlockSpec(a.shape, lambda i: (0,) * a.ndim)
    kern = functools.partial(_gla_kernel, c=c, levels=levels, heads=heads, dk=dk, dv=dv)
    return pl.pallas_call(
        kern,
        grid=(t // c,),
        in_specs=[col(0), col(1), col(2), pl.BlockSpec((c, lr.shape[1]), lambda i: (i, 0)), const2(gw),
                  pl.BlockSpec((1, kw), lambda i: (0, 0)), pl.BlockSpec((1, vw), lambda i: (0, 0)),
                  const2(wexp), const2(roles), const2(masks)],
        out_specs=pl.BlockSpec((c, vw), lambda i: (i, 0)),
        out_shape=jax.ShapeDtypeStruct((t, vw), BF16),
        scratch_shapes=[pltpu.VMEM((heads, dv, dk), F32)],
        compiler_params=_cparams(("arbitrary",), 32),
    )(p, p, p, lr, gw, gate_b.reshape(1, kw), norm_g.reshape(1, vw), wexp, roles, masks)


def _sb_kernel(q_ref, k_ref, v_ref, u_ref, o_ref, kb_ref, vb_ref, acc_ref, c_ref, *, tq, kb, scale):
    i = pl.program_id(1)

    @pl.when(i == 0)
    def _():
        kb_ref[...] = k_ref[...].astype(BF16)
        vb_ref[...] = v_ref[...].astype(BF16)

    qs = (q_ref[...] * scale).astype(BF16)
    acc_ref[...] = jnp.zeros_like(acc_ref)
    c_ref[...] = jnp.zeros_like(c_ref)
    t_idx = i * tq + lax.broadcasted_iota(jnp.int32, (tq, kb), 0)
    col = lax.broadcasted_iota(jnp.int32, (tq, kb), 1)
    u = u_ref[...]

    def cond(carry):
        j, cmax = carry
        return jnp.logical_and(j >= 0, cmax > SB_SKIP_LOG)

    def body(carry):
        j, _ = carry
        ks = pl.ds(pl.multiple_of(j * kb, kb), kb)
        z = _dot_nt(qs, kb_ref[ks, :])
        mask = (j * kb + col) < t_idx
        sp = jnp.maximum(z, 0.0) + jnp.log(1.0 + jnp.exp(-jnp.abs(z)))
        lk = jnp.where(mask, -sp, 0.0)
        l_hi, l_lo = _split_bf16(lk)
        r = _dot(l_hi, u) + _dot(l_lo, u)
        c = c_ref[...]
        w = jnp.where(mask, jnp.exp(z - sp + r + c), 0.0)
        acc_ref[...] += _dot(w.astype(BF16), vb_ref[ks, :])
        c_new = c + r[:, 0:1] + lk[:, 0:1]
        c_ref[...] = c_new
        return j - 1, jnp.max(c_new)

    lax.while_loop(cond, body, ((i * tq + tq - 1) // kb, jnp.float32(0.0)))
    o_ref[...] = acc_ref[...].astype(o_ref.dtype)


def _sb_attention(p, col0):
    t = p.shape[0]
    dh = SB_HEAD_DIM
    vw = (p.shape[1] - col0) // 3
    heads = vw // dh
    tq = min(SB_Q_BLOCK, t)
    kb = min(SB_K_BLOCK, t)
    qb, kbk, vbk = col0 // dh, (col0 + vw) // dh, (col0 + 2 * vw) // dh
    u = jnp.asarray(np.arange(kb)[:, None] > np.arange(kb)[None, :], BF16)
    kern = functools.partial(_sb_kernel, tq=tq, kb=kb, scale=dh ** -0.5)
    return pl.pallas_call(
        kern,
        grid=(heads, t // tq),
        in_specs=[pl.BlockSpec((tq, dh), lambda h, i: (i, qb + h)),
                  pl.BlockSpec((t, dh), lambda h, i: (0, kbk + h)),
                  pl.BlockSpec((t, dh), lambda h, i: (0, vbk + h)),
                  pl.BlockSpec((kb, kb), lambda h, i: (0, 0))],
        out_specs=pl.BlockSpec((tq, dh), lambda h, i: (i, h)),
        out_shape=jax.ShapeDtypeStruct((t, vw), BF16),
        scratch_shapes=[pltpu.VMEM((t, dh), BF16), pltpu.VMEM((t, dh), BF16),
                        pltpu.VMEM((tq, dh), F32), pltpu.VMEM((tq, 1), F32)],
        compiler_params=_cparams(("arbitrary", "arbitrary"), 40),
    )(p, p, p, u)


def _outproj_ln_kernel(a_ref, b_ref, w_ref, x_ref, gate_ref, g_ref, bb_ref, o_ref, *, alpha):
    half = a_ref.shape[1]
    y = _dot(a_ref[...], w_ref[0:half, :]) + _dot(b_ref[...], w_ref[half:, :])
    r = alpha * x_ref[...] + (1.0 + gate_ref[...]) * y
    o_ref[...] = _layer_norm(r, g_ref[...], bb_ref[...])


def _outproj_ln(mix_a, mix_b, w, x, gate, ln_g, ln_b, alpha):
    t, d = x.shape
    tm = min(256, t)
    half = mix_a.shape[1]
    row = pl.BlockSpec((1, d), lambda i: (0, 0))
    return pl.pallas_call(
        functools.partial(_outproj_ln_kernel, alpha=alpha),
        grid=(t // tm,),
        in_specs=[pl.BlockSpec((tm, half), lambda i: (i, 0)), pl.BlockSpec((tm, half), lambda i: (i, 0)),
                  pl.BlockSpec((d, d), lambda i: (0, 0)), pl.BlockSpec((tm, d), lambda i: (i, 0)), row, row, row],
        out_specs=pl.BlockSpec((tm, d), lambda i: (i, 0)),
        out_shape=jax.ShapeDtypeStruct((t, d), F32),
        compiler_params=_cparams(("arbitrary",), 48),
    )(mix_a, mix_b, w, x, gate, ln_g.reshape(1, d), ln_b.reshape(1, d))


def _router_kernel(x_ref, sc_ref, sh_ref, w_ref, b_ref, tri_ref, u_ref, info_ref, cnt_ref, carry_ref):
    @pl.when(pl.program_id(0) == 0)
    def _():
        carry_ref[...] = jnp.zeros_like(carry_ref)

    u = x_ref[...] * (1.0 + sc_ref[...]) + sh_ref[...]
    u_hi, u_lo = _split_bf16(u)
    u_ref[...] = u_hi
    w_hi, w_lo = _split_bf16(w_ref[...])
    logits = _dot(u_hi, w_hi) + _dot(u_lo, w_hi) + _dot(u_hi, w_lo) + b_ref[...]
    tm = logits.shape[0]
    lane_i = lax.broadcasted_iota(jnp.int32, (tm, LANES), 1)
    lane = lane_i.astype(F32)
    neg = -jnp.inf
    far = float(LANES)

    is_g = lane_i < N_GROUPS
    lg = jnp.where(is_g, logits, neg)
    mg = jnp.max(lg, axis=1, keepdims=True)
    zg = jnp.sum(jnp.where(is_g, jnp.exp(lg - mg), 0.0), axis=1, keepdims=True)
    g_p = 1.0 / zg
    g_idx = jnp.min(jnp.where(lg == mg, lane, far), axis=1, keepdims=True)

    e_grp = ((lane_i - N_GROUPS) // EXPERTS_PER_GROUP).astype(F32)
    is_e = (lane_i >= N_GROUPS) & (lane_i < N_GROUPS + N_EXPERTS) & (e_grp == g_idx)
    le = jnp.where(is_e, logits, neg)
    me = jnp.max(le, axis=1, keepdims=True)
    ee = jnp.where(is_e, jnp.exp(le - me), 0.0)
    pe = ee / jnp.sum(ee, axis=1, keepdims=True)
    cand = jnp.where(is_e, pe, -1.0)
    p1 = jnp.max(cand, axis=1, keepdims=True)
    i1 = jnp.min(jnp.where(cand == p1, lane, far), axis=1, keepdims=True)
    cand2 = jnp.where(lane == i1, -1.0, cand)
    p2 = jnp.max(cand2, axis=1, keepdims=True)
    i2 = jnp.min(jnp.where(cand2 == p2, lane, far), axis=1, keepdims=True)
    den = p1 + p2
    w1 = g_p * p1 / den
    w2 = g_p * p2 / den
    e1 = i1 - float(N_GROUPS)
    e2 = i2 - float(N_GROUPS)

    oh1 = (lane == e1).astype(F32)
    oh2 = (lane == e2).astype(F32)
    both = oh1 + oh2
    before = _dot(tri_ref[...], both.astype(BF16)) + carry_ref[...]
    r1 = jnp.sum(before * oh1, axis=1, keepdims=True)
    r2 = jnp.sum(before * oh2, axis=1, keepdims=True)
    carry_ref[...] += jnp.sum(both, axis=0, keepdims=True)
    cnt_ref[...] = carry_ref[...]

    info = jnp.zeros((tm, LANES), F32)
    for k, val in enumerate((e1, e2, w1, w2, r1, r2)):
        info = jnp.where(lane_i == k, val, info)
    info_ref[...] = info


def _router(x, scale, shift, w_rg, b_rg, w_re, b_re):
    n, d = x.shape
    tm = min(256, n)
    pad = LANES - N_GROUPS - N_EXPERTS
    w = jnp.concatenate([w_rg, w_re, jnp.zeros((d, pad), F32)], axis=1)
    b = jnp.concatenate([b_rg, b_re, jnp.zeros((pad,), F32)]).reshape(1, LANES)
    tri = jnp.asarray(np.arange(tm)[:, None] > np.arange(tm)[None, :], BF16)
    row = pl.BlockSpec((1, d), lambda i: (0, 0))
    return pl.pallas_call(
        _router_kernel,
        grid=(n // tm,),
        in_specs=[pl.BlockSpec((tm, d), lambda i: (i, 0)), row, row,
                  pl.BlockSpec((d, LANES), lambda i: (0, 0)), pl.BlockSpec((1, LANES), lambda i: (0, 0)),
                  pl.BlockSpec((tm, tm), lambda i: (0, 0))],
        out_specs=[pl.BlockSpec((tm, d), lambda i: (i, 0)), pl.BlockSpec((tm, LANES), lambda i: (i, 0)),
                   pl.BlockSpec((1, LANES), lambda i: (0, 0))],
        out_shape=[jax.ShapeDtypeStruct((n, d), BF16), jax.ShapeDtypeStruct((n, LANES), F32),
                   jax.ShapeDtypeStruct((1, LANES), F32)],
        scratch_shapes=[pltpu.VMEM((1, LANES), F32)],
        compiler_params=_cparams(("arbitrary",), 32),
    )(x, scale, shift, w, b, tri)


def _expert_kernel(be_ref, nu_ref, xs_ref, wg_ref, wu_ref, wd_ref, o_ref, wgb_ref, wub_ref, wdb_ref):
    b = pl.program_id(0)
    changed = jnp.logical_or(b == 0, be_ref[b] != be_ref[jnp.maximum(b - 1, 0)])

    @pl.when(changed)
    def _():
        wgb_ref[...] = wg_ref[0].astype(BF16)
        wub_ref[...] = wu_ref[0].astype(BF16)
        wdb_ref[...] = wd_ref[0].astype(BF16)

    @pl.when(b < nu_ref[0])
    def _():
        x = xs_ref[...]
        hid = _silu(_dot(x, wgb_ref[...])) * _dot(x, wub_ref[...])
        o_ref[...] = _dot(hid.astype(BF16), wdb_ref[...])


def _expert_mlp(xs, block_e, n_used, w_gate, w_up, w_down):
    rows, d = xs.shape
    hid = w_gate.shape[2]
    nb = rows // MOE_ROW_BLOCK
    xmap = lambda b, be, nu: (jnp.minimum(b, nu[0] - 1), 0)
    grid_spec = pltpu.PrefetchScalarGridSpec(
        num_scalar_prefetch=2,
        grid=(nb,),
        in_specs=[pl.BlockSpec((MOE_ROW_BLOCK, d), xmap),
                  pl.BlockSpec((1, d, hid), lambda b, be, nu: (be[b], 0, 0)),
                  pl.BlockSpec((1, d, hid), lambda b, be, nu: (be[b], 0, 0)),
                  pl.BlockSpec((1, hid, d), lambda b, be, nu: (be[b], 0, 0))],
        out_specs=pl.BlockSpec((MOE_ROW_BLOCK, d), xmap),
        scratch_shapes=[pltpu.VMEM((d, hid), BF16), pltpu.VMEM((d, hid), BF16), pltpu.VMEM((hid, d), BF16)],
    )
    return pl.pallas_call(
        _expert_kernel,
        grid_spec=grid_spec,
        out_shape=jax.ShapeDtypeStruct((rows, d), F32),
        compiler_params=_cparams(("arbitrary",), 48),
    )(block_e, n_used, xs, w_gate, w_up, w_down)


def _combine_ln_kernel(y0_ref, y1_ref, info_ref, x_ref, gate_ref, g_ref, bb_ref, o_ref, *, alpha):
    y = y0_ref[...] * info_ref[:, 2:3] + y1_ref[...] * info_ref[:, 3:4]
    r = alpha * x_ref[...] + (1.0 + gate_ref[...]) * y
    o_ref[...] = _layer_norm(r, g_ref[...], bb_ref[...])


def _combine_ln(y0, y1, info, x, gate, ln_g, ln_b, alpha):
    t, d = x.shape
    tm = min(512, t)
    row = pl.BlockSpec((1, d), lambda i: (0, 0))
    big = pl.BlockSpec((tm, d), lambda i: (i, 0))
    return pl.pallas_call(
        functools.partial(_combine_ln_kernel, alpha=alpha),
        grid=(t // tm,),
        in_specs=[big, big, pl.BlockSpec((tm, LANES), lambda i: (i, 0)), big, row, row, row],
        out_specs=big,
        out_shape=jax.ShapeDtypeStruct((t, d), F32),
        compiler_params=_cparams(("arbitrary",), 48),
    )(y0, y1, info, x, gate, ln_g.reshape(1, d), ln_b.reshape(1, d))


def _moe(x, scale, shift, gate, ln_g, ln_b, w_rg, b_rg, w_re, b_re, w_gate, w_up, w_down, alpha):
    n, d = x.shape
    u, info, cnt = _router(x, scale, shift, w_rg, b_rg, w_re, b_re)
    e = info[:, 0:2].astype(jnp.int32)
    rank = info[:, 4:6].astype(jnp.int32)
    counts = cnt[0, :N_EXPERTS].astype(jnp.int32)
    padded = (counts + MOE_ROW_BLOCK - 1) // MOE_ROW_BLOCK * MOE_ROW_BLOCK
    pad_end = jnp.cumsum(padded)
    dest = (pad_end - padded)[e] + rank
    slots = n * 2
    nb = (slots + N_EXPERTS * (MOE_ROW_BLOCK - 1) + MOE_ROW_BLOCK - 1) // MOE_ROW_BLOCK
    block_e = jnp.minimum(jnp.searchsorted(pad_end, jnp.arange(nb, dtype=jnp.int32) * MOE_ROW_BLOCK, side='right'),
                          N_EXPERTS - 1).astype(jnp.int32)
    n_used = (pad_end[-1:] // MOE_ROW_BLOCK).astype(jnp.int32)
    tok = jnp.broadcast_to(jnp.arange(n, dtype=jnp.int32)[:, None], (n, 2))
    rows_tok = jnp.zeros((nb * MOE_ROW_BLOCK,), jnp.int32).at[dest.reshape(-1)].set(tok.reshape(-1))
    xs = u[rows_tok]
    yb = _expert_mlp(xs, block_e, n_used, w_gate, w_up, w_down)
    return _combine_ln(yb[dest[:, 0]], yb[dest[:, 1]], info, x, gate, ln_g, ln_b, alpha)


def kernel(x, c, ada_w, ada_b, ln_g, ln_b, w_in_ab, pool_w, pool_scale, hgrn_lb_logits, hgrn_norm_g, w_out_ab, w_in_cd, gla_gate_w, gla_gate_b, gla_norm_g, w_out_cd, router_group_w, router_group_b, router_expert_w, router_expert_b, expert_w_gate, expert_w_up, expert_w_down):
    batch, t, d = x.shape
    assert batch == 1, "the recurrent state is carried across row blocks of a single sequence"
    depth = ada_w.shape[0]
    alpha = (2 * depth) ** 0.25
    pw = pool_scale.shape[1]
    kw = gla_gate_w.shape[2]
    vw = gla_norm_g.shape[1]
    rank = gla_gate_w.shape[1]

    mod = _ada_mod(c, ada_w, ada_b)
    h = x.reshape(t, d)
    for layer in range(depth):
        shift_m, scale_m, gate_m, shift_f, scale_f, gate_f = [mod[layer, :, k * d:(k + 1) * d] for k in range(6)]
        i = layer // 2
        if layer % 2 == 0:
            p = _mod_matmul(h, scale_m, shift_m, w_in_ab[i].astype(BF16))
            mix_a = _pool_mixer(p, pool_w[i], pool_scale[i])
            mix_b = _hgrn_mixer(p, hgrn_lb_logits, hgrn_norm_g[i], i)
            w_out = w_out_ab[i]
        else:
            w = w_in_cd[i]
            lr0 = 2 * kw + 2 * vw
            w_main = jnp.concatenate([w[:, :lr0], w[:, lr0 + rank:]], axis=1).astype(BF16)
            w_lr = jnp.zeros((d, LANES), BF16).at[:, :rank].set(w[:, lr0:lr0 + rank].astype(BF16))
            p, lr = _mod_matmul(h, scale_m, shift_m, w_main, w_lr)
            mix_a = _gla_mixer(p, lr, gla_gate_w[i], gla_gate_b[i], gla_norm_g[i])
            mix_b = _sb_attention(p, lr0)
            w_out = w_out_cd[i]
        h = _outproj_ln(mix_a, mix_b, w_out.astype(BF16), h, gate_m, ln_g[layer, 0], ln_b[layer, 0], alpha)
        h = _moe(h, scale_f, shift_f, gate_f, ln_g[layer, 1], ln_b[layer, 1], router_group_w[layer],
                 router_group_b[layer], router_expert_w[layer], router_expert_b[layer], expert_w_gate[layer],
                 expert_w_up[layer], expert_w_down[layer], alpha)
    return h.reshape(batch, t, d)
```

```python
import functools
import math

import numpy as np
import jax
import jax.numpy as jnp
from jax import lax
from jax.experimental import pallas as pl
from jax.experimental.pallas import tpu as pltpu

F32 = jnp.float32
BF16 = jnp.bfloat16

LANES = 128
SUBLANES = 8
MIB = 1024 * 1024

POOL_WINDOWS = (2, 4, 8, 16)
POOL_HALO = 16
HGRN_HEAD_DIM = 128
GLA_HEADS = 4
GLA_GATE_TEMP = 16.0
SB_HEAD_DIM = 128
N_GROUPS = 4
EXPERTS_PER_GROUP = 8
N_EXPERTS = N_GROUPS * EXPERTS_PER_GROUP
TOP_K = 2
MOE_ROW_BLOCK = 128
NORM_EPS = 1e-5

GLA_CHUNK = 64
SB_BLOCK = 256
SB_SKIP_LOG = -110.0
ADA_ROWS = 64
IN_PROJ_TN = 512
MOE_TOKEN_TILE = 256


def _cparams(semantics, vmem_mib):
    return pltpu.CompilerParams(dimension_semantics=semantics, vmem_limit_bytes=vmem_mib * MIB)


def _dot(a, b):
    return jnp.dot(a, b, preferred_element_type=F32)


def _dot_nt(a, b):
    return lax.dot_general(a, b, (((1,), (1,)), ((), ())), preferred_element_type=F32)


def _dot_tn(a, b):
    return lax.dot_general(a, b, (((0,), (0,)), ((), ())), preferred_element_type=F32)


def _split_bf16(a):
    hi = a.astype(BF16)
    lo = (a - hi.astype(F32)).astype(BF16)
    return hi, lo


def _silu(a):
    return a * jax.nn.sigmoid(a)


def _layer_norm(r, g, b):
    mu = jnp.mean(r, axis=-1, keepdims=True)
    xc = r - mu
    var = jnp.mean(xc * xc, axis=-1, keepdims=True)
    return xc * lax.rsqrt(var + NORM_EPS) * g + b


def _ada_mod_kernel(c_ref, w_ref, b_ref, o_ref):
    d = c_ref.shape[0]
    bn = o_ref.shape[-1]

    def body(r, acc):
        rows = pl.ds(pl.multiple_of(r * ADA_ROWS, ADA_ROWS), ADA_ROWS)
        cond = _silu(c_ref[rows, :])
        p = cond * w_ref[0, rows, :]
        return acc + jnp.sum(p.reshape(ADA_ROWS // SUBLANES, SUBLANES, bn), axis=0)

    acc = lax.fori_loop(0, d // ADA_ROWS, body, jnp.zeros((SUBLANES, bn), F32))
    o_ref[0] = jnp.sum(acc, axis=0, keepdims=True) + b_ref[0]


def _ada_mod(c, ada_w, ada_b):
    depth, d, n6 = ada_w.shape
    bn = 1024
    return pl.pallas_call(
        _ada_mod_kernel,
        grid=(depth, n6 // bn),
        in_specs=[pl.BlockSpec((d, 1), lambda l, j: (0, 0)),
                  pl.BlockSpec((1, d, bn), lambda l, j: (l, 0, j)),
                  pl.BlockSpec((1, 1, bn), lambda l, j: (l, 0, j))],
        out_specs=pl.BlockSpec((1, 1, bn), lambda l, j: (l, 0, j)),
        out_shape=jax.ShapeDtypeStruct((depth, 1, n6), F32),
        compiler_params=_cparams(("arbitrary", "arbitrary"), 32),
        name="ada_mod",
    )(c.reshape(d, 1), ada_w, ada_b.reshape(depth, 1, n6))


def _in_proj_ab_kernel(x_ref, sc_ref, sh_ref, w_ref, o_ref, of_ref, u_ref, *, j0, j1):
    j = pl.program_id(1)

    @pl.when(j == 0)
    def _():
        u_ref[...] = (x_ref[...] * (1.0 + sc_ref[...]) + sh_ref[...]).astype(BF16)

    acc = _dot(u_ref[...], w_ref[...])
    o_ref[...] = acc.astype(o_ref.dtype)

    @pl.when(jnp.logical_and(j >= j0, j < j1))
    def _():
        of_ref[...] = acc


def _in_proj_ab(x, scale, shift, w, f_col0, f_width):
    m, d = x.shape
    n = w.shape[1]
    tm = min(1024, m)
    tn = IN_PROJ_TN
    j0, j1 = f_col0 // tn, (f_col0 + f_width) // tn
    row = pl.BlockSpec((1, d), lambda i, j: (0, 0))
    return pl.pallas_call(
        functools.partial(_in_proj_ab_kernel, j0=j0, j1=j1),
        grid=(m // tm, n // tn),
        in_specs=[pl.BlockSpec((tm, d), lambda i, j: (i, 0)), row, row,
                  pl.BlockSpec((d, tn), lambda i, j: (0, j))],
        out_specs=[pl.BlockSpec((tm, tn), lambda i, j: (i, j)),
                   pl.BlockSpec((tm, tn), lambda i, j: (i, jnp.clip(j - j0, 0, j1 - j0 - 1)))],
        out_shape=[jax.ShapeDtypeStruct((m, n), BF16), jax.ShapeDtypeStruct((m, f_width), F32)],
        scratch_shapes=[pltpu.VMEM((tm, d), BF16)],
        compiler_params=_cparams(("arbitrary", "arbitrary"), 48),
        name="in_proj_ab",
    )(x, scale, shift, w)


def _in_proj_cd_kernel(x_ref, sc_ref, sh_ref, w_ref, w2_ref, o_ref, o2_ref, u_ref):
    @pl.when(pl.program_id(1) == 0)
    def _():
        u = (x_ref[...] * (1.0 + sc_ref[...]) + sh_ref[...]).astype(BF16)
        u_ref[...] = u
        o2_ref[...] = _dot(u, w2_ref[...])

    o_ref[...] = _dot(u_ref[...], w_ref[...]).astype(o_ref.dtype)


def _in_proj_cd(x, scale, shift, w, w2):
    m, d = x.shape
    n = w.shape[1]
    n2 = w2.shape[1]
    tm = min(1024, m)
    tn = IN_PROJ_TN
    row = pl.BlockSpec((1, d), lambda i, j: (0, 0))
    return pl.pallas_call(
        _in_proj_cd_kernel,
        grid=(m // tm, n // tn),
        in_specs=[pl.BlockSpec((tm, d), lambda i, j: (i, 0)), row, row,
                  pl.BlockSpec((d, tn), lambda i, j: (0, j)), pl.BlockSpec((d, n2), lambda i, j: (0, 0))],
        out_specs=[pl.BlockSpec((tm, tn), lambda i, j: (i, j)), pl.BlockSpec((tm, n2), lambda i, j: (i, 0))],
        out_shape=[jax.ShapeDtypeStruct((m, n), BF16), jax.ShapeDtypeStruct((m, n2), F32)],
        scratch_shapes=[pltpu.VMEM((tm, d), BF16)],
        compiler_params=_cparams(("arbitrary", "arbitrary"), 48),
        name="in_proj_cd",
    )(x, scale, shift, w, w2)


def _pool_kernel(a_ref, halo_ref, pw_ref, ps_ref, o_ref, buf_ref, *, tm, gd, sub):
    i = pl.program_id(0)
    buf_ref[0:POOL_HALO, :] = jnp.where(i > 0, halo_ref[...].astype(F32), 0.0)
    buf_ref[POOL_HALO:POOL_HALO + tm, :] = a_ref[...].astype(F32)
    for r0 in range(0, tm, sub):
        pos = (i * tm + r0 + 1 + lax.broadcasted_iota(jnp.int32, (sub, 1), 0)).astype(F32)
        for g, w in enumerate(POOL_WINDOWS):
            cols = slice(g * gd, (g + 1) * gd)
            base = POOL_HALO + r0
            a = buf_ref[base:base + sub, cols]
            s = a
            for k in range(1, w):
                s = s + buf_ref[base - k:base - k + sub, cols]
            pooled = s / jnp.minimum(pos, float(w)) - a
            y = _dot(pooled.astype(BF16), pw_ref[g].astype(BF16))
            o_ref[r0:r0 + sub, cols] = (y * ps_ref[:, cols]).astype(o_ref.dtype)


def _pool_mixer(p, pool_w, pool_scale):
    t = p.shape[0]
    groups, gd, _ = pool_w.shape
    width = groups * gd
    tm = min(512, t)
    sub = 128
    kern = functools.partial(_pool_kernel, tm=tm, gd=gd, sub=sub)
    return pl.pallas_call(
        kern,
        grid=(t // tm,),
        in_specs=[pl.BlockSpec((tm, width), lambda i: (i, 0)),
                  pl.BlockSpec((POOL_HALO, width), lambda i: (jnp.maximum(i * (tm // POOL_HALO) - 1, 0), 0)),
                  pl.BlockSpec((groups, gd, gd), lambda i: (0, 0, 0)),
                  pl.BlockSpec((1, width), lambda i: (0, 0))],
        out_specs=pl.BlockSpec((tm, width), lambda i: (i, 0)),
        out_shape=jax.ShapeDtypeStruct((t, width), BF16),
        scratch_shapes=[pltpu.VMEM((POOL_HALO + tm, width), F32)],
        compiler_params=_cparams(("arbitrary",), 32),
        name="pool_mixer",
    )(p, p, pool_w, pool_scale.reshape(1, width))


@functools.lru_cache(maxsize=None)
def _gla_constants(c):
    levels = int(math.log2(c))
    idx = np.arange(c)
    r = idx[None, :]
    i = idx[:, None]
    mats = [r <= i, r > i]
    roles, masks = [], []
    for l in range(levels):
        s = 1 << l
        mid = ((idx // (2 * s)) * (2 * s) + s)[:, None]
        upper = (idx % (2 * s)) >= s
        mats.append(np.where(upper[:, None], (r >= mid) & (r <= i), (r > i) & (r < mid)))
        roles.append(np.broadcast_to(upper[:, None], (c, LANES)))
        same = (idx[:, None] // (2 * s)) == (idx[None, :] // (2 * s))
        masks.append(same & upper[:, None] & ~upper[None, :])
    masks.append(np.eye(c, dtype=bool))
    wexp = np.concatenate(mats, axis=0).astype(np.float32)
    wexp = np.concatenate([wexp, wexp], axis=1)
    return (jnp.asarray(wexp, BF16), jnp.asarray(np.stack(roles), F32), jnp.asarray(np.stack(masks), F32))


def _chunk_gla(q, k, v, g, st_ref, h, wexp, roles, masks, c, levels):
    g_hi, g_lo = _split_bf16(g)
    ex = jnp.exp(_dot(wexp, jnp.concatenate([g_hi, g_lo], axis=0)))
    eb = ex[0:c]
    er = ex[c:2 * c]
    vb = v.astype(BF16)
    a = jnp.where(masks[levels], _dot_nt(q.astype(BF16), k.astype(BF16)), 0.0)
    for l in range(levels):
        xl = (jnp.where(roles[l], q, k) * ex[(2 + l) * c:(3 + l) * c]).astype(BF16)
        a = a + jnp.where(masks[l], _dot_nt(xl, xl), 0.0)
    st = st_ref[h]
    o = _dot_nt((q * eb).astype(BF16), st.astype(BF16)) + _dot(a.astype(BF16), vb)
    st_ref[h] = st * eb[c - 1:c, :] + _dot_tn(vb, (k * er).astype(BF16))
    return o


def _gated_rms(o, norm_g, gate):
    return o * lax.rsqrt(jnp.mean(o * o, axis=-1, keepdims=True) + NORM_EPS) * norm_g * _silu(gate)


def _hgrn_kernel(q_ref, f_ref, i_ref, go_ref, lbl_ref, ng_ref, wexp_ref, roles_ref, masks_ref, o_ref, st_ref,
                 *, layer_i, c, levels, heads, dh):
    @pl.when(pl.program_id(0) == 0)
    def _():
        st_ref[...] = jnp.zeros_like(st_ref)

    rows = [lbl_ref[r:r + 1, :] for r in range(lbl_ref.shape[0])]
    mx = functools.reduce(jnp.maximum, rows)
    es = [jnp.exp(r - mx) for r in rows]
    tot = functools.reduce(jnp.add, es)
    sm = [e / tot for e in es]
    cum = [functools.reduce(jnp.add, sm[:r + 1]) for r in range(len(sm))]
    lb = cum[layer_i] - cum[0]
    log_lb = jnp.log(lb)
    log_1m = jnp.log1p(-lb)
    one_m = 1.0 - lb

    wexp = wexp_ref[...]
    roles = [roles_ref[l] > 0.5 for l in range(levels)]
    masks = [masks_ref[l] > 0.5 for l in range(levels + 1)]
    for h in range(heads):
        hs = slice(h * dh, (h + 1) * dh)
        z = f_ref[:, hs]
        g = jnp.logaddexp(log_lb[:, hs], log_1m[:, hs] + jax.nn.log_sigmoid(z))
        k = one_m[:, hs] * jax.nn.sigmoid(-z)
        q = _silu(q_ref[:, hs].astype(F32))
        o = _chunk_gla(q, k, i_ref[:, hs], g, st_ref, h, wexp, roles, masks, c, levels)
        o_ref[:, hs] = _gated_rms(o, ng_ref[:, hs], go_ref[:, hs].astype(F32)).astype(o_ref.dtype)


def _hgrn_mixer(p, pf, lb_logits, norm_g, layer_i):
    t = p.shape[0]
    width = lb_logits.shape[1]
    heads = width // HGRN_HEAD_DIM
    c = GLA_CHUNK
    levels = int(math.log2(c))
    wexp, roles, masks = _gla_constants(c)
    col = lambda k: pl.BlockSpec((c, width), lambda i: (i, k))
    const2 = lambda a: pl.BlockSpec(a.shape, lambda i: (0,) * a.ndim)
    kern = functools.partial(_hgrn_kernel, layer_i=layer_i, c=c, levels=levels, heads=heads, dh=HGRN_HEAD_DIM)
    return pl.pallas_call(
        kern,
        grid=(t // c,),
        in_specs=[col(1), col(0), col(3), col(4), const2(lb_logits), pl.BlockSpec((1, width), lambda i: (0, 0)),
                  const2(wexp), const2(roles), const2(masks)],
        out_specs=pl.BlockSpec((c, width), lambda i: (i, 0)),
        out_shape=jax.ShapeDtypeStruct((t, width), BF16),
        scratch_shapes=[pltpu.VMEM((heads, HGRN_HEAD_DIM, HGRN_HEAD_DIM), F32)],
        compiler_params=_cparams(("arbitrary",), 32),
        name="hgrn_mixer",
    )(p, pf, p, p, lb_logits, norm_g.reshape(1, width), wexp, roles, masks)


def _gla_kernel(qk_ref, v_ref, go_ref, lr_ref, gw_ref, gb_ref, ng_ref, wexp_ref, roles_ref, masks_ref, o_ref, st_ref,
                *, c, levels, heads, dk, dv):
    @pl.when(pl.program_id(0) == 0)
    def _():
        st_ref[...] = jnp.zeros_like(st_ref)

    kw = heads * dk
    pre = _dot(lr_ref[...].astype(BF16), gw_ref[...].astype(BF16)) + gb_ref[...]
    log_a = jax.nn.log_sigmoid(pre) / GLA_GATE_TEMP
    wexp = wexp_ref[...]
    roles = [roles_ref[l] > 0.5 for l in range(levels)]
    masks = [masks_ref[l] > 0.5 for l in range(levels + 1)]
    for h in range(heads):
        ks = slice(h * dk, (h + 1) * dk)
        vs = slice(h * dv, (h + 1) * dv)
        q = qk_ref[:, ks].astype(F32) * (dk ** -0.5)
        k = qk_ref[:, kw + h * dk:kw + (h + 1) * dk].astype(F32)
        o = _chunk_gla(q, k, v_ref[:, vs], log_a[:, ks], st_ref, h, wexp, roles, masks, c, levels)
        o_ref[:, vs] = _gated_rms(o, ng_ref[:, vs], go_ref[:, vs].astype(F32)).astype(o_ref.dtype)


def _gla_mixer(p, lr, gate_w, gate_b, norm_g):
    t = p.shape[0]
    vw = norm_g.shape[0]
    kw = gate_w.shape[1]
    heads = GLA_HEADS
    dk, dv = kw // heads, vw // heads
    c = GLA_CHUNK
    levels = int(math.log2(c))
    wexp, roles, masks = _gla_constants(c)
    gw = jnp.zeros((lr.shape[1], kw), F32).at[:gate_w.shape[0]].set(gate_w)
    col = lambda k: pl.BlockSpec((c, vw), lambda i: (i, k))
    const2 = lambda a: pl.BlockSpec(a.shape, lambda i: (0,) * a.ndim)
    kern = functools.partial(_gla_kernel, c=c, levels=levels, heads=heads, dk=dk, dv=dv)
    return pl.pallas_call(
        kern,
        grid=(t // c,),
        in_specs=[col(0), col(1), col(2), pl.BlockSpec((c, lr.shape[1]), lambda i: (i, 0)), const2(gw),
                  pl.BlockSpec((1, kw), lambda i: (0, 0)), pl.BlockSpec((1, vw), lambda i: (0, 0)),
                  const2(wexp), const2(roles), const2(masks)],
        out_specs=pl.BlockSpec((c, vw), lambda i: (i, 0)),
        out_shape=jax.ShapeDtypeStruct((t, vw), BF16),
        scratch_shapes=[pltpu.VMEM((heads, dv, dk), F32)],
        compiler_params=_cparams(("arbitrary",), 32),
        name="gla_mixer",
    )(p, p, p, lr, gw, gate_b.reshape(1, kw), norm_g.reshape(1, vw), wexp, roles, masks)


def _sb_tile(qs, kj, vj, u, c, mask):
    z = _dot_nt(qs, kj)
    sp = jnp.maximum(z, 0.0) + jnp.log(1.0 + jnp.exp(-jnp.abs(z)))
    lk = -sp if mask is None else jnp.where(mask, -sp, 0.0)
    l_hi, l_lo = _split_bf16(lk)
    r = _dot(l_hi, u) + _dot(l_lo, u)
    w = jnp.exp(z - sp + r + c)
    if mask is not None:
        w = jnp.where(mask, w, 0.0)
    return _dot(w.astype(BF16), vj), c + r[:, 0:1] + lk[:, 0:1]


def _sb_kernel(q_ref, k_ref, v_ref, u_ref, o_ref, acc_ref, c_ref, *, tb, heads, dh, scale):
    i = pl.program_id(0)
    u = u_ref[...]
    qs = [(q_ref[:, h * dh:(h + 1) * dh].astype(F32) * scale).astype(BF16) for h in range(heads)]
    row = lax.broadcasted_iota(jnp.int32, (tb, tb), 0)
    col = lax.broadcasted_iota(jnp.int32, (tb, tb), 1)
    diag_mask = col < row

    ks = pl.ds(pl.multiple_of(i * tb, tb), tb)
    cmax = jnp.float32(-jnp.inf)
    for h in range(heads):
        hs = slice(h * dh, (h + 1) * dh)
        o, c = _sb_tile(qs[h], k_ref[ks, hs], v_ref[ks, hs], u, jnp.zeros((tb, 1), F32), diag_mask)
        acc_ref[:, hs] = o
        c_ref[h] = c
        cmax = jnp.maximum(cmax, jnp.max(c))

    def cond(carry):
        j, cm = carry
        return jnp.logical_and(j >= 0, cm > SB_SKIP_LOG)

    def body(carry):
        j, _ = carry
        kj = pl.ds(pl.multiple_of(j * tb, tb), tb)
        cm = jnp.float32(-jnp.inf)
        for h in range(heads):
            hs = slice(h * dh, (h + 1) * dh)
            o, c = _sb_tile(qs[h], k_ref[kj, hs], v_ref[kj, hs], u, c_ref[h], None)
            acc_ref[:, hs] += o
            c_ref[h] = c
            cm = jnp.maximum(cm, jnp.max(c))
        return j - 1, cm

    lax.while_loop(cond, body, (i - 1, cmax))
    o_ref[...] = acc_ref[...].astype(o_ref.dtype)


def _sb_attention(p, col0):
    t = p.shape[0]
    dh = SB_HEAD_DIM
    vw = (p.shape[1] - col0) // 3
    heads = vw // dh
    tb = min(SB_BLOCK, t)
    cb = col0 // vw
    u = jnp.asarray(np.arange(tb)[:, None] > np.arange(tb)[None, :], BF16)
    kern = functools.partial(_sb_kernel, tb=tb, heads=heads, dh=dh, scale=dh ** -0.5)
    resident = lambda k: pl.BlockSpec((t, vw), lambda i: (0, k), pipeline_mode=pl.Buffered(1))
    return pl.pallas_call(
        kern,
        grid=(t // tb,),
        in_specs=[pl.BlockSpec((tb, vw), lambda i: (i, cb)), resident(cb + 1), resident(cb + 2),
                  pl.BlockSpec((tb, tb), lambda i: (0, 0))],
        out_specs=pl.BlockSpec((tb, vw), lambda i: (i, 0)),
        out_shape=jax.ShapeDtypeStruct((t, vw), BF16),
        scratch_shapes=[pltpu.VMEM((tb, vw), F32), pltpu.VMEM((heads, tb, 1), F32)],
        compiler_params=_cparams(("arbitrary",), 52),
        name="sb_attention",
    )(p, p, p, u)


def _outproj_ln_kernel(a_ref, b_ref, w_ref, x_ref, gate_ref, g_ref, bb_ref, o_ref, *, alpha):
    half = a_ref.shape[1]
    y = _dot(a_ref[...], w_ref[0:half, :]) + _dot(b_ref[...], w_ref[half:, :])
    r = alpha * x_ref[...] + (1.0 + gate_ref[...]) * y
    o_ref[...] = _layer_norm(r, g_ref[...], bb_ref[...])


def _outproj_ln(mix_a, mix_b, w, x, gate, ln_g, ln_b, alpha):
    t, d = x.shape
    tm = min(256, t)
    half = mix_a.shape[1]
    row = pl.BlockSpec((1, d), lambda i: (0, 0))
    return pl.pallas_call(
        functools.partial(_outproj_ln_kernel, alpha=alpha),
        grid=(t // tm,),
        in_specs=[pl.BlockSpec((tm, half), lambda i: (i, 0)), pl.BlockSpec((tm, half), lambda i: (i, 0)),
                  pl.BlockSpec((d, d), lambda i: (0, 0)), pl.BlockSpec((tm, d), lambda i: (i, 0)), row, row, row],
        out_specs=pl.BlockSpec((tm, d), lambda i: (i, 0)),
        out_shape=jax.ShapeDtypeStruct((t, d), F32),
        compiler_params=_cparams(("arbitrary",), 48),
        name="outproj_ln",
    )(mix_a, mix_b, w, x, gate, ln_g.reshape(1, d), ln_b.reshape(1, d))


def _router_kernel(x_ref, sc_ref, sh_ref, w_ref, b_ref, tri_ref, u_ref, info_ref, cnt_ref, carry_ref):
    @pl.when(pl.program_id(0) == 0)
    def _():
        carry_ref[...] = jnp.zeros_like(carry_ref)

    u = x_ref[...] * (1.0 + sc_ref[...]) + sh_ref[...]
    u_ref[...] = u
    u_hi, u_lo = _split_bf16(u)
    w_hi, w_lo = _split_bf16(w_ref[...])
    logits = _dot(u_hi, w_hi) + _dot(u_lo, w_hi) + _dot(u_hi, w_lo) + b_ref[...]
    tm = logits.shape[0]
    lane_i = lax.broadcasted_iota(jnp.int32, (tm, LANES), 1)
    lane = lane_i.astype(F32)
    neg = -jnp.inf
    far = float(LANES)

    is_g = lane_i < N_GROUPS
    lg = jnp.where(is_g, logits, neg)
    mg = jnp.max(lg, axis=1, keepdims=True)
    zg = jnp.sum(jnp.where(is_g, jnp.exp(lg - mg), 0.0), axis=1, keepdims=True)
    g_p = 1.0 / zg
    g_idx = jnp.min(jnp.where(lg == mg, lane, far), axis=1, keepdims=True)

    e_grp = ((lane_i - N_GROUPS) // EXPERTS_PER_GROUP).astype(F32)
    is_e = (lane_i >= N_GROUPS) & (lane_i < N_GROUPS + N_EXPERTS) & (e_grp == g_idx)
    le = jnp.where(is_e, logits, neg)
    me = jnp.max(le, axis=1, keepdims=True)
    ee = jnp.where(is_e, jnp.exp(le - me), 0.0)
    pe = ee / jnp.sum(ee, axis=1, keepdims=True)
    cand = jnp.where(is_e, pe, -1.0)
    p1 = jnp.max(cand, axis=1, keepdims=True)
    i1 = jnp.min(jnp.where(cand == p1, lane, far), axis=1, keepdims=True)
    cand2 = jnp.where(lane == i1, -1.0, cand)
    p2 = jnp.max(cand2, axis=1, keepdims=True)
    i2 = jnp.min(jnp.where(cand2 == p2, lane, far), axis=1, keepdims=True)
    den = p1 + p2
    w1 = g_p * p1 / den
    w2 = g_p * p2 / den
    e1 = i1 - float(N_GROUPS)
    e2 = i2 - float(N_GROUPS)

    oh1 = (lane == e1).astype(F32)
    oh2 = (lane == e2).astype(F32)
    both = oh1 + oh2
    before = _dot(tri_ref[...], both.astype(BF16)) + carry_ref[...]
    r1 = jnp.sum(before * oh1, axis=1, keepdims=True)
    r2 = jnp.sum(before * oh2, axis=1, keepdims=True)
    carry_ref[...] += jnp.sum(both, axis=0, keepdims=True)
    cnt_ref[...] = carry_ref[...]

    info = jnp.zeros((tm, LANES), F32)
    for k, val in enumerate((e1, e2, w1, w2, r1, r2)):
        info = jnp.where(lane_i == k, val, info)
    info_ref[...] = info


def _router(x, scale, shift, w_rg, b_rg, w_re, b_re):
    n, d = x.shape
    tm = min(256, n)
    pad = LANES - N_GROUPS - N_EXPERTS
    w = jnp.concatenate([w_rg, w_re, jnp.zeros((d, pad), F32)], axis=1)
    b = jnp.concatenate([b_rg, b_re, jnp.zeros((pad,), F32)]).reshape(1, LANES)
    tri = jnp.asarray(np.arange(tm)[:, None] > np.arange(tm)[None, :], BF16)
    row = pl.BlockSpec((1, d), lambda i: (0, 0))
    return pl.pallas_call(
        _router_kernel,
        grid=(n // tm,),
        in_specs=[pl.BlockSpec((tm, d), lambda i: (i, 0)), row, row,
                  pl.BlockSpec((d, LANES), lambda i: (0, 0)), pl.BlockSpec((1, LANES), lambda i: (0, 0)),
                  pl.BlockSpec((tm, tm), lambda i: (0, 0))],
        out_specs=[pl.BlockSpec((tm, d), lambda i: (i, 0)), pl.BlockSpec((tm, LANES), lambda i: (i, 0)),
                   pl.BlockSpec((1, LANES), lambda i: (0, 0))],
        out_shape=[jax.ShapeDtypeStruct((n, d), F32), jax.ShapeDtypeStruct((n, LANES), F32),
                   jax.ShapeDtypeStruct((1, LANES), F32)],
        scratch_shapes=[pltpu.VMEM((1, LANES), F32)],
        compiler_params=_cparams(("arbitrary",), 32),
        name="router",
    )(x, scale, shift, w, b, tri)


def _dispatch_kernel(dest_ref, fill_ref, u_ref, xs_ref, zero_ref, sem, fsem, *, tm, first_tail, nb):
    i = pl.program_id(0)

    def fill_copy(start):
        return pltpu.make_async_copy(zero_ref, xs_ref.at[pl.ds(pl.multiple_of(start, MOE_ROW_BLOCK), MOE_ROW_BLOCK)],
                                     fsem)

    @pl.when(i == 0)
    def _():
        zero_ref[...] = jnp.zeros_like(zero_ref)
        n_used = fill_ref[N_EXPERTS]
        for e in range(N_EXPERTS):
            fill_copy(fill_ref[e]).start()
        for b in range(first_tail, nb):
            @pl.when(b >= n_used)
            def _():
                fill_copy(b * MOE_ROW_BLOCK).start()
        for e in range(N_EXPERTS):
            fill_copy(fill_ref[e]).wait()
        for b in range(first_tail, nb):
            @pl.when(b >= n_used)
            def _():
                fill_copy(b * MOE_ROW_BLOCK).wait()

    def row_copy(r, k):
        return pltpu.make_async_copy(u_ref.at[pl.ds(r, 1)],
                                     xs_ref.at[pl.ds(dest_ref[(i * tm + r) * TOP_K + k], 1)], sem)

    for r in range(tm):
        for k in range(TOP_K):
            row_copy(r, k).start()
    for r in range(tm):
        for k in range(TOP_K):
            row_copy(r, k).wait()


def _dispatch(u, dest, fill, rows):
    n, d = u.shape
    tm = min(MOE_TOKEN_TILE, n)
    nb = rows // MOE_ROW_BLOCK
    first_tail = n * TOP_K // MOE_ROW_BLOCK
    grid_spec = pltpu.PrefetchScalarGridSpec(
        num_scalar_prefetch=2,
        grid=(n // tm,),
        in_specs=[pl.BlockSpec((tm, d), lambda i, dest, fill: (i, 0))],
        out_specs=pl.BlockSpec(memory_space=pl.ANY),
        scratch_shapes=[pltpu.VMEM((MOE_ROW_BLOCK, d), F32), pltpu.SemaphoreType.DMA(()),
                        pltpu.SemaphoreType.DMA(())],
    )
    return pl.pallas_call(
        functools.partial(_dispatch_kernel, tm=tm, first_tail=first_tail, nb=nb),
        grid_spec=grid_spec,
        out_shape=jax.ShapeDtypeStruct((rows, d), F32),
        compiler_params=_cparams(("arbitrary",), 32),
        name="moe_dispatch",
    )(dest, fill, u)


def _expert_kernel(be_ref, nu_ref, xs_ref, wg_ref, wu_ref, wd_ref, o_ref, wgb_ref, wub_ref, wdb_ref):
    b = pl.program_id(0)
    changed = jnp.logical_or(b == 0, be_ref[b] != be_ref[jnp.maximum(b - 1, 0)])

    @pl.when(changed)
    def _():
        wgb_ref[...] = wg_ref[0].astype(BF16)
        wub_ref[...] = wu_ref[0].astype(BF16)
        wdb_ref[...] = wd_ref[0].astype(BF16)

    @pl.when(b < nu_ref[0])
    def _():
        x = xs_ref[...].astype(BF16)
        hid = _silu(_dot(x, wgb_ref[...])) * _dot(x, wub_ref[...])
        o_ref[...] = _dot(hid.astype(BF16), wdb_ref[...])

    @pl.when(b >= nu_ref[0])
    def _():
        o_ref[...] = jnp.zeros_like(o_ref)


def _expert_mlp(xs, block_e, n_used, w_gate, w_up, w_down):
    rows, d = xs.shape
    hid = w_gate.shape[2]
    nb = rows // MOE_ROW_BLOCK
    xmap = lambda b, be, nu: (jnp.minimum(b, nu[0] - 1), 0)
    grid_spec = pltpu.PrefetchScalarGridSpec(
        num_scalar_prefetch=2,
        grid=(nb,),
        in_specs=[pl.BlockSpec((MOE_ROW_BLOCK, d), xmap),
                  pl.BlockSpec((1, d, hid), lambda b, be, nu: (be[b], 0, 0)),
                  pl.BlockSpec((1, d, hid), lambda b, be, nu: (be[b], 0, 0)),
                  pl.BlockSpec((1, hid, d), lambda b, be, nu: (be[b], 0, 0))],
        out_specs=pl.BlockSpec((MOE_ROW_BLOCK, d), lambda b, be, nu: (b, 0)),
        scratch_shapes=[pltpu.VMEM((d, hid), BF16), pltpu.VMEM((d, hid), BF16), pltpu.VMEM((hid, d), BF16)],
    )
    return pl.pallas_call(
        _expert_kernel,
        grid_spec=grid_spec,
        out_shape=jax.ShapeDtypeStruct((rows, d), F32),
        compiler_params=_cparams(("arbitrary",), 48),
        name="expert_mlp",
    )(block_e, n_used, xs, w_gate, w_up, w_down)


def _combine_ln_kernel(dest_ref, yb_ref, info_ref, x_ref, gate_ref, g_ref, bb_ref, o_ref, ybuf_ref, sem, *, tm, alpha):
    i = pl.program_id(0)
    n = pl.num_programs(0)

    def row_copy(tile, slot, r, k):
        return pltpu.make_async_copy(yb_ref.at[pl.ds(dest_ref[(tile * tm + r) * TOP_K + k], 1)],
                                     ybuf_ref.at[slot, k, pl.ds(r, 1)], sem.at[slot])

    def fetch(tile, slot):
        for r in range(tm):
            for k in range(TOP_K):
                row_copy(tile, slot, r, k).start()

    @pl.when(i == 0)
    def _():
        fetch(0, 0)

    @pl.when(i + 1 < n)
    def _():
        fetch(i + 1, (i + 1) % 2)

    slot = i % 2
    for r in range(tm):
        for k in range(TOP_K):
            row_copy(i, slot, r, k).wait()

    y = ybuf_ref[slot, 0] * info_ref[:, 2:3] + ybuf_ref[slot, 1] * info_ref[:, 3:4]
    r = alpha * x_ref[...] + (1.0 + gate_ref[...]) * y
    o_ref[...] = _layer_norm(r, g_ref[...], bb_ref[...])


def _combine_ln(yb, dest, info, x, gate, ln_g, ln_b, alpha):
    t, d = x.shape
    tm = min(MOE_TOKEN_TILE, t)
    row = pl.BlockSpec((1, d), lambda i, dest: (0, 0))
    big = pl.BlockSpec((tm, d), lambda i, dest: (i, 0))
    grid_spec = pltpu.PrefetchScalarGridSpec(
        num_scalar_prefetch=1,
        grid=(t // tm,),
        in_specs=[pl.BlockSpec(memory_space=pl.ANY), pl.BlockSpec((tm, LANES), lambda i, dest: (i, 0)), big,
                  row, row, row],
        out_specs=big,
        scratch_shapes=[pltpu.VMEM((2, TOP_K, tm, d), F32), pltpu.SemaphoreType.DMA((2,))],
    )
    return pl.pallas_call(
        functools.partial(_combine_ln_kernel, tm=tm, alpha=alpha),
        grid_spec=grid_spec,
        out_shape=jax.ShapeDtypeStruct((t, d), F32),
        compiler_params=_cparams(("arbitrary",), 40),
        name="combine_ln",
    )(dest, yb, info, x, gate, ln_g.reshape(1, d), ln_b.reshape(1, d))


def _moe(x, scale, shift, gate, ln_g, ln_b, w_rg, b_rg, w_re, b_re, w_gate, w_up, w_down, layer, alpha):
    n, d = x.shape
    u, info, cnt = _router(x, scale, shift, w_rg, b_rg, w_re, b_re)
    e = info[:, 0:TOP_K].astype(jnp.int32)
    rank = info[:, 4:4 + TOP_K].astype(jnp.int32)
    counts = cnt[0, :N_EXPERTS].astype(jnp.int32)
    padded = (counts + MOE_ROW_BLOCK - 1) // MOE_ROW_BLOCK * MOE_ROW_BLOCK
    pad_end = jnp.cumsum(padded)
    pad_start = pad_end - padded
    ids = jnp.arange(N_EXPERTS, dtype=jnp.int32)
    dest = jnp.sum(jnp.where(e[..., None] == ids, pad_start, 0), axis=-1) + rank
    slots = n * TOP_K
    nb = (slots + N_EXPERTS * (MOE_ROW_BLOCK - 1) + MOE_ROW_BLOCK - 1) // MOE_ROW_BLOCK
    rows = nb * MOE_ROW_BLOCK
    block_start = jnp.arange(nb, dtype=jnp.int32) * MOE_ROW_BLOCK
    block_e = jnp.minimum(jnp.sum((pad_end[None, :] <= block_start[:, None]).astype(jnp.int32), axis=1),
                          N_EXPERTS - 1) + layer * N_EXPERTS
    n_used = pad_end[-1:] // MOE_ROW_BLOCK
    fill = jnp.concatenate([jnp.maximum(pad_end - MOE_ROW_BLOCK, 0), n_used])
    dest = dest.reshape(slots)
    xs = _dispatch(u, dest, fill, rows)
    yb = _expert_mlp(xs, block_e, n_used, w_gate, w_up, w_down)
    return _combine_ln(yb, dest, info, x, gate, ln_g, ln_b, alpha)


def kernel(x, c, ada_w, ada_b, ln_g, ln_b, w_in_ab, pool_w, pool_scale, hgrn_lb_logits, hgrn_norm_g, w_out_ab, w_in_cd, gla_gate_w, gla_gate_b, gla_norm_g, w_out_cd, router_group_w, router_group_b, router_expert_w, router_expert_b, expert_w_gate, expert_w_up, expert_w_down):
    batch, t, d = x.shape
    assert batch == 1, "the recurrent state is carried across row blocks of a single sequence"
    depth = ada_w.shape[0]
    alpha = (2 * depth) ** 0.25
    pw = pool_scale.shape[1]
    hw = hgrn_lb_logits.shape[1]
    assert pw == hw, "the even-layer column blocks are addressed in units of one branch width"
    kw = gla_gate_w.shape[2]
    vw = gla_norm_g.shape[1]
    rank = gla_gate_w.shape[1]
    hid = expert_w_gate.shape[-1]
    w_gate = expert_w_gate.reshape(depth * N_EXPERTS, d, hid)
    w_up = expert_w_up.reshape(depth * N_EXPERTS, d, hid)
    w_down = expert_w_down.reshape(depth * N_EXPERTS, hid, d)

    mod = _ada_mod(c, ada_w, ada_b)
    h = x.reshape(t, d)
    for layer in range(depth):
        shift_m, scale_m, gate_m, shift_f, scale_f, gate_f = [mod[layer, :, k * d:(k + 1) * d] for k in range(6)]
        i = layer // 2
        if layer % 2 == 0:
            p, pf = _in_proj_ab(h, scale_m, shift_m, w_in_ab[i].astype(BF16), pw + hw, hw)
            mix_a = _pool_mixer(p, pool_w[i], pool_scale[i])
            mix_b = _hgrn_mixer(p, pf, hgrn_lb_logits, hgrn_norm_g[i], i)
            w_out = w_out_ab[i]
        else:
            w = w_in_cd[i]
            lr0 = 2 * kw + 2 * vw
            w_main = jnp.concatenate([w[:, :lr0], w[:, lr0 + rank:]], axis=1).astype(BF16)
            w_lr = jnp.zeros((d, LANES), BF16).at[:, :rank].set(w[:, lr0:lr0 + rank].astype(BF16))
            p, lr = _in_proj_cd(h, scale_m, shift_m, w_main, w_lr)
            mix_a = _gla_mixer(p, lr, gla_gate_w[i], gla_gate_b[i], gla_norm_g[i])
            mix_b = _sb_attention(p, lr0)
            w_out = w_out_cd[i]
        h = _outproj_ln(mix_a, mix_b, w_out.astype(BF16), h, gate_m, ln_g[layer, 0], ln_b[layer, 0], alpha)
        h = _moe(h, scale_f, shift_f, gate_f, ln_g[layer, 1], ln_b[layer, 1], router_group_w[layer],
                 router_group_b[layer], router_expert_w[layer], router_expert_b[layer], w_gate, w_up, w_down,
                 layer, alpha)
    return h.reshape(batch, t, d)
```
